```python
import math
import jax, jax.numpy as jnp
from jax import lax
import numpy as np

D_MODEL = 1024
BATCH = 1
SEQ = 16384
DEPTH = 2
DEC_BATCH = 8
DEC_SEQ = 8192
PAST_LEN = 128

HEAD_DIM = 64
GRID_W = 64
PLE_DIM = 256
EPS = 1e-6
BLOCK = 128
A_HEADS = 8
A_KV = 2
A_WINDOW = 128
B_HEADS = 8
NB_ROWS_MAX = 8
NB_COLS = 16
NB_QCOLS = 16
NB_KCOLS = 32
C_HEADS = 8
C_KV = 2
ROPE_THETA = 10000.0
D_HEADS = 4
D_VDIM = 2 * HEAD_DIM

A_W = A_HEADS * HEAD_DIM
B_W = B_HEADS * HEAD_DIM
C_W = C_HEADS * HEAD_DIM
D_W = D_HEADS * D_VDIM
D_QK = D_HEADS * 2 * HEAD_DIM
MIX_W = A_W + B_W
AB_SIZES = [A_W, A_KV * HEAD_DIM, A_KV * HEAD_DIM, A_W, B_W, B_W, B_W, B_W]
CD_SIZES = [C_W, C_KV * HEAD_DIM, C_KV * HEAD_DIM, C_W, D_QK, D_QK, D_W, D_W]
AB_WIDTH = sum(AB_SIZES)
CD_WIDTH = sum(CD_SIZES)
AB_SPLITS = [int(v) for v in np.cumsum(AB_SIZES)[:-1]]
CD_SPLITS = [int(v) for v in np.cumsum(CD_SIZES)[:-1]]
N_EVEN = (DEPTH + 1) // 2
N_ODD = DEPTH // 2

kernel_name = "hybrid_window_neighbourhood_axial_diff_encoder"


def rmsnorm(x, g):
    xf = x.astype(jnp.float32)
    y = xf * lax.rsqrt(jnp.mean(xf * xf, axis=-1, keepdims=True) + EPS)
    return (y * g.astype(jnp.float32)).astype(x.dtype)


def alibi_slopes(n):
    return (2.0 ** (-8.0 * np.arange(1, n + 1) / n)).astype(np.float32)


def lambda_init_for(layer):
    return 0.8 - 0.6 * math.exp(-0.3 * layer)


def window_attention(q, k, v, sink):
    B, S, H, d = q.shape
    KV = k.shape[2]
    G = H // KV
    nb = S // BLOCK
    span = 3 * BLOCK
    kp = jnp.pad(k, ((0, 0), (BLOCK, BLOCK), (0, 0), (0, 0)))
    vp = jnp.pad(v, ((0, 0), (BLOCK, BLOCK), (0, 0), (0, 0)))
    qb = q.reshape(B, nb, BLOCK, KV, G, d).transpose(1, 0, 2, 3, 4, 5)
    rel = np.arange(span)[None, :] - BLOCK - np.arange(BLOCK)[:, None]
    band = jnp.asarray(np.abs(rel) <= A_WINDOW)
    alibi = jnp.asarray(-alibi_slopes(H).reshape(KV, G, 1, 1) * np.abs(rel).astype(np.float32))
    sink32 = sink.astype(jnp.float32).reshape(KV, G)[:, :, None]
    scale = HEAD_DIM ** -0.5

    def one_block(args):
        q_blk, n = args
        k_blk = lax.dynamic_slice_in_dim(kp, n * BLOCK, span, axis=1)
        v_blk = lax.dynamic_slice_in_dim(vp, n * BLOCK, span, axis=1)
        kpos = n * BLOCK - BLOCK + jnp.arange(span)
        ok = band & ((kpos >= 0) & (kpos < S))[None, :]
        s = jnp.einsum('bqhgd,bchd->bhgqc', q_blk, k_blk).astype(jnp.float32) * scale + alibi
        s = jnp.where(ok, s, -jnp.inf)
        m = jnp.maximum(jnp.max(s, axis=-1), sink32)
        e = jnp.exp(s - m[..., None])
        denom = jnp.sum(e, axis=-1) + jnp.exp(sink32 - m)
        w = (e / denom[..., None]).astype(v.dtype)
        return jnp.einsum('bhgqc,bchd->bqhgd', w, v_blk)

    out = lax.map(one_block, (qb, jnp.arange(nb)))
    return out.transpose(1, 0, 2, 3, 4, 5).reshape(B, S, H, d)


def neighbourhood_attention(q, k, v, rpb):
    B, S, H, d = q.shape
    rows = S // GRID_W
    kr = min(NB_ROWS_MAX, rows)
    n_cb = GRID_W // NB_QCOLS
    r = np.arange(rows)
    row_start = np.clip(r - kr // 2, 0, rows - kr)
    qcol = np.arange(GRID_W).reshape(n_cb, NB_QCOLS)
    win_start = np.clip(qcol - NB_COLS // 2, 0, GRID_W - NB_COLS)
    kcol0 = np.clip(np.arange(n_cb) * NB_QCOLS - NB_COLS // 2, 0, GRID_W - NB_KCOLS)
    kcol = kcol0[:, None] + np.arange(NB_KCOLS)
    col_ok = (kcol[:, None, :] >= win_start[..., None]) & (kcol[:, None, :] < win_start[..., None] + NB_COLS)
    key_mask = jnp.asarray(np.broadcast_to(col_ok[:, :, None, :], (n_cb, NB_QCOLS, kr, NB_KCOLS))
                           .reshape(n_cb, NB_QCOLS, kr * NB_KCOLS))[:, None]
    dc = jnp.asarray(np.clip(kcol[:, None, :] - qcol[..., None], -(NB_COLS - 1), NB_COLS - 1)
                     + NB_COLS - 1, dtype=jnp.int32)
    dr = jnp.asarray(row_start[:, None] + np.arange(kr)[None, :] - r[:, None] + NB_ROWS_MAX - 1,
                     dtype=jnp.int32)
    kcol_j = jnp.asarray(kcol, dtype=jnp.int32)
    rpb32 = rpb.astype(jnp.float32)
    kg = k.reshape(B, rows, GRID_W, H, d)
    vg = v.reshape(B, rows, GRID_W, H, d)
    qg = q.reshape(B, rows, n_cb, NB_QCOLS, H, d).transpose(1, 0, 2, 3, 4, 5)
    scale = HEAD_DIM ** -0.5
    n_keys = kr * NB_KCOLS

    def one_row(args):
        q_row, r0, dr_row = args
        krows = lax.dynamic_slice_in_dim(kg, r0, kr, axis=1)
        vrows = lax.dynamic_slice_in_dim(vg, r0, kr, axis=1)
        kb = jnp.take(krows, kcol_j, axis=2).transpose(0, 2, 1, 3, 4, 5).reshape(B, n_cb, n_keys, H, d)
        vb = jnp.take(vrows, kcol_j, axis=2).transpose(0, 2, 1, 3, 4, 5).reshape(B, n_cb, n_keys, H, d)
        s = jnp.einsum('bmqhd,bmkhd->bmhqk', q_row, kb).astype(jnp.float32) * scale
        bias = rpb32[:, dr_row[None, None, :, None], dc[:, :, None, :]]
        bias = bias.reshape(H, n_cb, NB_QCOLS, n_keys).transpose(1, 0, 2, 3)
        s = jnp.where(key_mask, s + bias, -jnp.inf)
        w = jax.nn.softmax(s, axis=-1).astype(v.dtype)
        return jnp.einsum('bmhqk,bmkhd->bmqhd', w, vb)

    out = lax.map(one_row, (qg, jnp.asarray(row_start, dtype=jnp.int32), dr))
    return out.transpose(1, 0, 2, 3, 4, 5).reshape(B, S, H, d)


def axial_rope(x, row, col):
    B, S, H, d = x.shape
    n_pair = d // 4
    inv = jnp.asarray((ROPE_THETA ** (-2.0 * np.arange(n_pair) / (d // 2))).astype(np.float32))
    ang = jnp.concatenate([row[:, None] * inv, col[:, None] * inv], axis=-1)
    cos = jnp.cos(ang)[None, :, None, :]
    sin = jnp.sin(ang)[None, :, None, :]
    xf = x.astype(jnp.float32).reshape(B, S, H, d // 2, 2)
    x0, x1 = xf[..., 0], xf[..., 1]
    out = jnp.stack([x0 * cos - x1 * sin, x0 * sin + x1 * cos], axis=-1).reshape(B, S, H, d)
    return out.astype(x.dtype)


def dense_gqa(q, k, v):
    B, S, H, d = q.shape
    KV = k.shape[2]
    G = H // KV
    nb = S // BLOCK
    qb = q.reshape(B, nb, BLOCK, KV, G, d).transpose(1, 0, 2, 3, 4, 5)
    scale = HEAD_DIM ** -0.5

    def one_block(q_blk):
        s = jnp.einsum('bqhgd,bshd->bhgqs', q_blk, k).astype(jnp.float32) * scale
        w = jax.nn.softmax(s, axis=-1).astype(v.dtype)
        return jnp.einsum('bhgqs,bshd->bqhgd', w, v)

    out = lax.map(one_block, qb)
    return out.transpose(1, 0, 2, 3, 4, 5).reshape(B, S, H, d)


def diff_attention(q, k, v, lam):
    B, S, H, _, d = q.shape
    nb = S // BLOCK
    qb = q.reshape(B, nb, BLOCK, H, 2, d).transpose(1, 0, 2, 3, 4, 5)
    slopes = jnp.asarray(alibi_slopes(H)).reshape(H, 1, 1, 1)
    kpos = jnp.arange(S)
    scale = HEAD_DIM ** -0.5

    def one_block(args):
        q_blk, n = args
        qpos = n * BLOCK + jnp.arange(BLOCK)
        dist = jnp.abs(qpos[:, None] - kpos[None, :]).astype(jnp.float32)
        s = jnp.einsum('bqhmd,bshmd->bhmqs', q_blk, k).astype(jnp.float32) * scale - slopes * dist
        a = jax.nn.softmax(s, axis=-1)
        w = (a[:, :, 0] - lam * a[:, :, 1]).astype(v.dtype)
        return jnp.einsum('bhqs,bshe->bqhe', w, v)

    out = lax.map(one_block, (qb, jnp.arange(nb)))
    return out.transpose(1, 0, 2, 3, 4).reshape(B, S, H, 2 * d)


def layer_ab(h, w_in, w_out, sink, rpb):
    B, S, _ = h.shape
    qa, ka, va, ga, qb, kb, vb, gb = jnp.split(h @ w_in, AB_SPLITS, axis=-1)
    oa = window_attention(qa.reshape(B, S, A_HEADS, HEAD_DIM), ka.reshape(B, S, A_KV, HEAD_DIM),
                          va.reshape(B, S, A_KV, HEAD_DIM), sink)
    ob = neighbourhood_attention(qb.reshape(B, S, B_HEADS, HEAD_DIM), kb.reshape(B, S, B_HEADS, HEAD_DIM),
                                 vb.reshape(B, S, B_HEADS, HEAD_DIM), rpb)
    y = jnp.concatenate([oa.reshape(B, S, A_W) * jax.nn.silu(ga),
                         ob.reshape(B, S, B_W) * jax.nn.silu(gb)], axis=-1)
    return y @ w_out


def layer_cd(h, w_in, w_out, q_norm, k_norm, lq1, lk1, lq2, lk2, subln, lam_init):
    B, S, _ = h.shape
    qc, kc, vc, gc, qd, kd, vd, gd = jnp.split(h @ w_in, CD_SPLITS, axis=-1)
    t = jnp.arange(S)
    row = (t // GRID_W).astype(jnp.float32)
    col = (t % GRID_W).astype(jnp.float32)
    qc = axial_rope(rmsnorm(qc.reshape(B, S, C_HEADS, HEAD_DIM), q_norm), row, col)
    kc = axial_rope(rmsnorm(kc.reshape(B, S, C_KV, HEAD_DIM), k_norm), row, col)
    oc = dense_gqa(qc, kc, vc.reshape(B, S, C_KV, HEAD_DIM))
    f32 = jnp.float32
    lam = (jnp.exp(jnp.sum(lq1.astype(f32) * lk1.astype(f32)))
           - jnp.exp(jnp.sum(lq2.astype(f32) * lk2.astype(f32))) + lam_init)
    od = diff_attention(qd.reshape(B, S, D_HEADS, 2, HEAD_DIM), kd.reshape(B, S, D_HEADS, 2, HEAD_DIM),
                        vd.reshape(B, S, D_HEADS, D_VDIM), lam)
    od = rmsnorm(od, subln) * (1.0 - lam_init)
    y = jnp.concatenate([oc.reshape(B, S, C_W) * jax.nn.silu(gc),
                         od.reshape(B, S, D_W) * jax.nn.silu(gd)], axis=-1)
    return y @ w_out


def trunk(x, p, norm_pre, norm_post, w_ple, w_ple_gate, w_in_ab, w_out_ab, a_sink, b_rpb,
          w_in_cd, w_out_cd, c_q_norm, c_k_norm, d_lambda_q1, d_lambda_k1, d_lambda_q2, d_lambda_k2, d_subln):
    for i in range(DEPTH):
        j = i // 2
        hn = rmsnorm(x, norm_pre[i])
        if i % 2 == 0:
            mix = layer_ab(hn, w_in_ab[j], w_out_ab[j], a_sink[j], b_rpb[j])
        else:
            mix = layer_cd(hn, w_in_cd[j], w_out_cd[j], c_q_norm[j], c_k_norm[j], d_lambda_q1[j],
                           d_lambda_k1[j], d_lambda_q2[j], d_lambda_k2[j], d_subln[j], lambda_init_for(i))
        x = x + rmsnorm(mix, norm_post[i])
        x = x + jax.nn.sigmoid(x @ w_ple_gate[i]) * (p[i] @ w_ple[i])
    return x


def setup_inputs(seed: int = 0) -> dict:
    key = jax.random.key(seed)
    ks = jax.random.split(key, 21)

    def nrm(k, shape, scale):
        return jax.random.normal(k, shape, jnp.float32) * scale

    return {
        "x_prompt": nrm(ks[0], (BATCH, SEQ, D_MODEL), 1.0),
        "x_sample": nrm(ks[1], (DEC_BATCH, DEC_SEQ, D_MODEL), 1.0),
        "p_prompt": nrm(ks[2], (DEPTH, BATCH, SEQ, PLE_DIM), 1.0),
        "p_sample": nrm(ks[3], (DEPTH, DEC_BATCH, DEC_SEQ, PLE_DIM), 1.0),
        "norm_pre": 1.0 + nrm(ks[4], (DEPTH, D_MODEL), 0.05),
        "norm_post": 1.0 + nrm(ks[5], (DEPTH, D_MODEL), 0.05),
        "w_ple": nrm(ks[6], (DEPTH, PLE_DIM, D_MODEL), PLE_DIM ** -0.5),
        "w_ple_gate": nrm(ks[7], (DEPTH, D_MODEL, D_MODEL), D_MODEL ** -0.5),
        "w_in_ab": nrm(ks[8], (N_EVEN, D_MODEL, AB_WIDTH), D_MODEL ** -0.5),
        "w_out_ab": nrm(ks[9], (N_EVEN, MIX_W, D_MODEL), MIX_W ** -0.5),
        "a_sink": nrm(ks[10], (N_EVEN, A_HEADS), 0.5),
        "b_rpb": nrm(ks[11], (N_EVEN, B_HEADS, 2 * NB_ROWS_MAX - 1, 2 * NB_COLS - 1), 0.1),
        "w_in_cd": nrm(ks[12], (N_ODD, D_MODEL, CD_WIDTH), D_MODEL ** -0.5),
        "w_out_cd": nrm(ks[13], (N_ODD, MIX_W, D_MODEL), MIX_W ** -0.5),
        "c_q_norm": 1.0 + nrm(ks[14], (N_ODD, HEAD_DIM), 0.05),
        "c_k_norm": 1.0 + nrm(ks[15], (N_ODD, HEAD_DIM), 0.05),
        "d_lambda_q1": nrm(ks[16], (N_ODD, HEAD_DIM), 0.1),
        "d_lambda_k1": nrm(ks[17], (N_ODD, HEAD_DIM), 0.1),
        "d_lambda_q2": nrm(ks[18], (N_ODD, HEAD_DIM), 0.1),
        "d_lambda_k2": nrm(ks[19], (N_ODD, HEAD_DIM), 0.1),
        "d_subln": 1.0 + nrm(ks[20], (N_ODD, D_VDIM), 0.05),
    }


def reference(x_prompt, x_sample, p_prompt, p_sample, norm_pre, norm_post, w_ple, w_ple_gate,
              w_in_ab, w_out_ab, a_sink, b_rpb, w_in_cd, w_out_cd, c_q_norm, c_k_norm,
              d_lambda_q1, d_lambda_k1, d_lambda_q2, d_lambda_k2, d_subln):
    y_prompt = trunk(x_prompt, p_prompt, norm_pre, norm_post, w_ple, w_ple_gate, w_in_ab, w_out_ab,
                     a_sink, b_rpb, w_in_cd, w_out_cd, c_q_norm, c_k_norm, d_lambda_q1, d_lambda_k1,
                     d_lambda_q2, d_lambda_k2, d_subln)
    y_sample = trunk(x_sample, p_sample, norm_pre, norm_post, w_ple, w_ple_gate, w_in_ab, w_out_ab,
                     a_sink, b_rpb, w_in_cd, w_out_cd, c_q_norm, c_k_norm, d_lambda_q1, d_lambda_k1,
                     d_lambda_q2, d_lambda_k2, d_subln)
    return (y_prompt, y_sample)
```

```python
import functools
import math

import numpy as np
import jax
import jax.numpy as jnp
from jax import lax
from jax.experimental import pallas as pl
from jax.experimental.pallas import tpu as pltpu

D_MODEL = 1024
HEAD_DIM = 64
PAIR = 2 * HEAD_DIM
GRID_W = 64
PLE_DIM = 256
EPS = 1e-6
ROPE_THETA = 10000.0
WINDOW = 128
NB_ROWS = 8
NB_COLS = 16
NB_WIN_ROWS = 16
A_HEADS = 8
B_HEADS = 8
C_HEADS = 8
D_HEADS = 4
PROJ_W = 3328
NEG = -1e30
VMEM_LIMIT = 56 * 1024 * 1024

COL_Q1, COL_G1, COL_Q2, COL_K2, COL_V2, COL_G2, COL_K1, COL_V1 = (
    0, 512, 1024, 1536, 2048, 2560, 3072, 3200)
GQ_PERM = (0, 4, 1, 5, 2, 6, 3, 7)


def _alibi_slopes(n):
    return [2.0 ** (-8.0 * (i + 1) / n) for i in range(n)]


def _lane_partner(y, shift):
    w = y.shape[-1]
    lane = lax.broadcasted_iota(jnp.int32, y.shape, 1)
    up = pltpu.roll(y, w - shift, axis=1)
    dn = pltpu.roll(y, shift, axis=1)
    return jnp.where((lane & shift) == 0, up, dn)


def _head_sum(y):
    for shift in (1, 2, 4, 8, 16, 32):
        y = y + _lane_partner(y, shift)
    return y


def _silu(g):
    return g * (1.0 / (1.0 + jnp.exp(-g)))


def _split_pair(qp):
    lo = lax.broadcasted_iota(jnp.int32, qp.shape, 1) < HEAD_DIM
    zero = jnp.zeros_like(qp)
    return jnp.concatenate([jnp.where(lo, qp, zero), jnp.where(lo, zero, qp)], axis=0)


def _merge_pair(o2):
    n = o2.shape[0] // 2
    lo = lax.broadcasted_iota(jnp.int32, (n, PAIR), 1) < HEAD_DIM
    return jnp.where(lo, o2[:n], o2[n:])


def _proj_kernel(x_ref, gpre_ref, w_ref, *rest, qk_rope):
    if qk_rope:
        cos_ref, sin_ref, gq_ref, gk_ref, out_ref = rest
    else:
        (out_ref,) = rest
    x = x_ref[0]
    r = lax.rsqrt(jnp.mean(x * x, axis=-1, keepdims=True) + EPS)
    h = (x * r * gpre_ref[...]).astype(jnp.bfloat16)

    def norm_rope(a, gain):
        ms = _head_sum(a * a) * (1.0 / HEAD_DIM)
        a = a * lax.rsqrt(ms + EPS) * gain
        cos = cos_ref[...]
        sin = sin_ref[...]
        outs = []
        for c in range(a.shape[1] // PAIR):
            ac = a[:, c * PAIR:(c + 1) * PAIR]
            outs.append(ac * cos + _lane_partner(ac, 1) * sin)
        return outs[0] if len(outs) == 1 else jnp.concatenate(outs, axis=1)

    for c0 in range(0, PROJ_W, 512):
        c1 = min(c0 + 512, PROJ_W)
        acc = jnp.dot(h, w_ref[:, c0:c1], preferred_element_type=jnp.float32)
        if qk_rope and c0 == COL_Q1:
            acc = norm_rope(acc, gq_ref[...])
        if qk_rope and c0 == COL_K1:
            k1 = norm_rope(acc[:, :PAIR], gk_ref[...])
            acc = jnp.concatenate([k1, acc[:, PAIR:]], axis=1)
        out_ref[0, :, c0:c1] = acc.astype(out_ref.dtype)


def _proj(x, gpre, w, rope=None):
    b, s, _ = x.shape
    tm = min(512, s)
    in_specs = [
        pl.BlockSpec((1, tm, D_MODEL), lambda i, j: (i, j, 0)),
        pl.BlockSpec((1, D_MODEL), lambda i, j: (0, 0)),
        pl.BlockSpec((D_MODEL, PROJ_W), lambda i, j: (0, 0)),
    ]
    args = [x, gpre, w]
    if rope is not None:
        cos, sin, gq, gk = rope
        in_specs += [
            pl.BlockSpec((tm, PAIR), lambda i, j: (j, 0)),
            pl.BlockSpec((tm, PAIR), lambda i, j: (j, 0)),
            pl.BlockSpec((1, 512), lambda i, j: (0, 0)),
            pl.BlockSpec((1, PAIR), lambda i, j: (0, 0)),
        ]
        args += [cos, sin, gq, gk]
    return pl.pallas_call(
        functools.partial(_proj_kernel, qk_rope=rope is not None),
        grid=(b, s // tm),
        in_specs=in_specs,
        out_specs=pl.BlockSpec((1, tm, PROJ_W), lambda i, j: (i, j, 0)),
        out_shape=jax.ShapeDtypeStruct((b, s, PROJ_W), jnp.bfloat16),
        compiler_params=pltpu.CompilerParams(
            dimension_semantics=("parallel", "parallel"),
            vmem_limit_bytes=VMEM_LIMIT),
        name="proj_rope" if rope is not None else "proj",
    )(*args)


def _window_kernel(sink_ref, q_ref, g_ref, kp_ref, kc_ref, kn_ref,
                   vp_ref, vc_ref, vn_ref, out_ref, *, seq):
    n = pl.program_id(1)
    blk = WINDOW
    k3 = jnp.concatenate([kp_ref[0], kc_ref[0], kn_ref[0]], axis=0)
    v3 = jnp.concatenate([vp_ref[0], vc_ref[0], vn_ref[0]], axis=0)
    row = lax.broadcasted_iota(jnp.int32, (2 * blk, 3 * blk), 0)
    col = lax.broadcasted_iota(jnp.int32, (2 * blk, 3 * blk), 1)
    qi = jnp.where(row < blk, row, row - blk)
    rel = col - blk - qi
    kpos = n * blk - blk + col
    ok = (jnp.abs(rel) <= WINDOW) & (kpos >= 0) & (kpos < seq)
    dist = jnp.abs(rel).astype(jnp.float32)
    top = lax.broadcasted_iota(jnp.int32, (2 * blk, 1), 0) < blk
    slopes = _alibi_slopes(A_HEADS)
    for p in range(4):
        h0, h1 = GQ_PERM[2 * p], GQ_PERM[2 * p + 1]
        q2 = _split_pair(q_ref[0, :, p * PAIR:(p + 1) * PAIR])
        s = lax.dot_general(q2, k3, (((1,), (1,)), ((), ())),
                            preferred_element_type=jnp.float32)
        slope = jnp.where(top, slopes[h0], slopes[h1])
        s = jnp.where(ok, s - slope * dist, NEG)
        sink = jnp.where(top, sink_ref[h0], sink_ref[h1])
        m = jnp.maximum(jnp.max(s, axis=-1, keepdims=True), sink)
        e = jnp.exp(s - m)
        denom = jnp.sum(e, axis=-1, keepdims=True) + jnp.exp(sink - m)
        o2 = jnp.dot(e.astype(jnp.bfloat16), v3, preferred_element_type=jnp.float32)
        o = _merge_pair(o2 / denom)
        g = g_ref[0, :, p * PAIR:(p + 1) * PAIR].astype(jnp.float32)
        out_ref[0, :, p * PAIR:(p + 1) * PAIR] = (o * _silu(g)).astype(out_ref.dtype)


def _window_attention(proj, sink):
    b, s, _ = proj.shape
    nb = s // WINDOW
    kcol, vcol = COL_K1 // PAIR, COL_V1 // PAIR

    def prev(i, j, sk):
        return (i, jnp.maximum(j - 1, 0), kcol)

    def nxt(i, j, sk):
        return (i, jnp.minimum(j + 1, nb - 1), kcol)

    def prev_v(i, j, sk):
        return (i, jnp.maximum(j - 1, 0), vcol)

    def nxt_v(i, j, sk):
        return (i, jnp.minimum(j + 1, nb - 1), vcol)

    kv = lambda f: pl.BlockSpec((1, WINDOW, PAIR), f)
    grid_spec = pltpu.PrefetchScalarGridSpec(
        num_scalar_prefetch=1,
        grid=(b, nb),
        in_specs=[
            pl.BlockSpec((1, WINDOW, 512), lambda i, j, sk: (i, j, COL_Q1 // 512)),
            pl.BlockSpec((1, WINDOW, 512), lambda i, j, sk: (i, j, COL_G1 // 512)),
            kv(prev), kv(lambda i, j, sk: (i, j, kcol)), kv(nxt),
            kv(prev_v), kv(lambda i, j, sk: (i, j, vcol)), kv(nxt_v),
        ],
        out_specs=pl.BlockSpec((1, WINDOW, 512), lambda i, j, sk: (i, j, 0)),
    )
    return pl.pallas_call(
        functools.partial(_window_kernel, seq=s),
        grid_spec=grid_spec,
        out_shape=jax.ShapeDtypeStruct((b, s, 512), jnp.bfloat16),
        compiler_params=pltpu.CompilerParams(
            dimension_semantics=("parallel", "parallel"),
            vmem_limit_bytes=VMEM_LIMIT),
        name="window_attn",
    )(sink, *([proj] * 8))


def _nbr_kernel(q_ref, g_ref, k_ref, v_ref, tab_ref, out_ref, *, rows):
    blk = pl.program_id(1)
    first = jnp.clip(8 * blk - 4, 0, rows - NB_WIN_ROWS)
    n_keys = NB_ROWS * GRID_W

    def one_row(j, carry):
        r = 8 * blk + j
        row_start = jnp.clip(r - NB_ROWS // 2, 0, rows - NB_ROWS)
        local = row_start - first
        shift = row_start - r + (NB_ROWS - 1)
        for p in range(4):
            lanes = slice(p * PAIR, (p + 1) * PAIR)
            q2 = _split_pair(q_ref[0, j, :, lanes])
            kw = k_ref[pl.ds(local, NB_ROWS), :, lanes].reshape(n_keys, PAIR)
            vw = v_ref[pl.ds(local, NB_ROWS), :, lanes].reshape(n_keys, PAIR)
            s = lax.dot_general(q2, kw, (((1,), (1,)), ((), ())),
                                preferred_element_type=jnp.float32)
            bias = jnp.concatenate([tab_ref[shift, 2 * p], tab_ref[shift, 2 * p + 1]], axis=0)
            s = s + bias
            m = jnp.max(s, axis=-1, keepdims=True)
            e = jnp.exp(s - m)
            denom = jnp.sum(e, axis=-1, keepdims=True)
            o2 = jnp.dot(e.astype(jnp.bfloat16), vw, preferred_element_type=jnp.float32)
            o = _merge_pair(o2 / denom)
            g = g_ref[0, j, :, lanes].astype(jnp.float32)
            out_ref[0, j, :, lanes] = (o * _silu(g)).astype(out_ref.dtype)
        return carry

    lax.fori_loop(0, 8, one_row, 0)


def _nbr_bias_table(rpb):
    qcol = np.arange(GRID_W)
    win = np.clip(qcol - NB_COLS // 2, 0, GRID_W - NB_COLS)
    kcol = np.arange(GRID_W)
    col_ok = (kcol[None, :] >= win[:, None]) & (kcol[None, :] < win[:, None] + NB_COLS)
    dc = np.clip(kcol[None, :] - qcol[:, None], -(NB_COLS - 1), NB_COLS - 1) + NB_COLS - 1
    shifts = np.arange(NB_ROWS)[:, None] + np.arange(NB_ROWS)[None, :]
    t = rpb.astype(jnp.float32)[:, shifts][:, :, :, dc]
    t = jnp.where(jnp.asarray(col_ok)[None, None, None], t, NEG)
    t = t.transpose(1, 0, 3, 2, 4)
    return t.reshape(NB_ROWS, B_HEADS, GRID_W, NB_ROWS * GRID_W)


def _nbr_attention(proj, table):
    b, s, _ = proj.shape
    rows = s // GRID_W
    assert rows % 8 == 0 and rows >= NB_WIN_ROWS
    proj4 = proj.reshape(b, rows, GRID_W, PROJ_W)

    def first(j):
        return jnp.clip(8 * j - 4, 0, rows - NB_WIN_ROWS)

    return pl.pallas_call(
        functools.partial(_nbr_kernel, rows=rows),
        grid=(b, rows // 8),
        in_specs=[
            pl.BlockSpec((1, 8, GRID_W, 512), lambda i, j: (i, j, 0, COL_Q2 // 512)),
            pl.BlockSpec((1, 8, GRID_W, 512), lambda i, j: (i, j, 0, COL_G2 // 512)),
            pl.BlockSpec((None, pl.Element(NB_WIN_ROWS), pl.Element(GRID_W), pl.Element(512)),
                         lambda i, j: (i, first(j), 0, COL_K2)),
            pl.BlockSpec((None, pl.Element(NB_WIN_ROWS), pl.Element(GRID_W), pl.Element(512)),
                         lambda i, j: (i, first(j), 0, COL_V2)),
            pl.BlockSpec((NB_ROWS, B_HEADS, GRID_W, NB_ROWS * GRID_W), lambda i, j: (0, 0, 0, 0)),
        ],
        out_specs=pl.BlockSpec((1, 8, GRID_W, 512), lambda i, j: (i, j, 0, 0)),
        out_shape=jax.ShapeDtypeStruct((b, rows, GRID_W, 512), jnp.bfloat16),
        compiler_params=pltpu.CompilerParams(
            dimension_semantics=("parallel", "parallel"),
            vmem_limit_bytes=VMEM_LIMIT),
        name="nbr_attn",
    )(proj4, proj4, proj4, proj4, table).reshape(b, s, 512)


def _flash_pair(q2, k_ref, v_ref, tk, bias_fn=None):
    rows = q2.shape[0]
    n_chunks = k_ref.shape[1] // tk

    def body(c, carry):
        m, l, acc = carry
        start = pl.multiple_of(c * tk, tk)
        kc = k_ref[0, pl.ds(start, tk), :]
        vc = v_ref[0, pl.ds(start, tk), :]
        s = lax.dot_general(q2, kc, (((1,), (1,)), ((), ())),
                            preferred_element_type=jnp.float32)
        if bias_fn is not None:
            s = s + bias_fn(start)
        m_new = jnp.maximum(m, jnp.max(s, axis=-1, keepdims=True))
        alpha = jnp.exp(m - m_new)
        e = jnp.exp(s - m_new)
        l = alpha * l + jnp.sum(e, axis=-1, keepdims=True)
        acc = alpha * acc + jnp.dot(e.astype(jnp.bfloat16), vc,
                                    preferred_element_type=jnp.float32)
        return m_new, l, acc

    init = (jnp.full((rows, 1), NEG, jnp.float32),
            jnp.zeros((rows, 1), jnp.float32),
            jnp.zeros((rows, PAIR), jnp.float32))
    _, l, acc = lax.fori_loop(0, n_chunks, body, init)
    return acc, l


def _gqa_kernel(q_ref, g_ref, k_ref, v_ref, out_ref, *, tk):
    for p in range(4):
        lanes = slice(p * PAIR, (p + 1) * PAIR)
        acc, l = _flash_pair(_split_pair(q_ref[0, :, lanes]), k_ref, v_ref, tk)
        o = _merge_pair(acc / l)
        g = g_ref[0, :, lanes].astype(jnp.float32)
        out_ref[0, :, lanes] = (o * _silu(g)).astype(out_ref.dtype)


def _gqa_attention(proj):
    b, s, _ = proj.shape
    tq = 128
    tk = min(512, s)
    return pl.pallas_call(
        functools.partial(_gqa_kernel, tk=tk),
        grid=(b, s // tq),
        in_specs=[
            pl.BlockSpec((1, tq, 512), lambda i, j: (i, j, COL_Q1 // 512)),
            pl.BlockSpec((1, tq, 512), lambda i, j: (i, j, COL_G1 // 512)),
            pl.BlockSpec((1, s, PAIR), lambda i, j: (i, 0, COL_K1 // PAIR)),
            pl.BlockSpec((1, s, PAIR), lambda i, j: (i, 0, COL_V1 // PAIR)),
        ],
        out_specs=pl.BlockSpec((1, tq, 512), lambda i, j: (i, j, 0)),
        out_shape=jax.ShapeDtypeStruct((b, s, 512), jnp.bfloat16),
        compiler_params=pltpu.CompilerParams(
            dimension_semantics=("parallel", "parallel"),
            vmem_limit_bytes=VMEM_LIMIT),
        name="gqa_attn",
    )(proj, proj, proj, proj)


def _diff_kernel(lam_ref, subln_ref, q_ref, g_ref, k_ref, v_ref, out_ref, *, tq, tk, lam_init):
    h = pl.program_id(1)
    qblk = pl.program_id(2)
    slopes = _alibi_slopes(D_HEADS)
    slope = jnp.float32(slopes[D_HEADS - 1])
    for i in range(D_HEADS - 2, -1, -1):
        slope = jnp.where(h == i, jnp.float32(slopes[i]), slope)
    row = lax.broadcasted_iota(jnp.int32, (2 * tq, tk), 0)
    col = lax.broadcasted_iota(jnp.int32, (2 * tq, tk), 1)
    rel = jnp.where(row < tq, row, row - tq) - col

    def bias(start):
        dist = jnp.abs(rel + (qblk * tq - start)).astype(jnp.float32)
        return -slope * dist

    acc, l = _flash_pair(_split_pair(q_ref[0]), k_ref, v_ref, tk, bias)
    a = acc / l
    lam = (jnp.exp(jnp.sum(lam_ref[0:1, :] * lam_ref[1:2, :], axis=-1, keepdims=True))
           - jnp.exp(jnp.sum(lam_ref[2:3, :] * lam_ref[3:4, :], axis=-1, keepdims=True))
           + lam_init)
    o = a[:tq] - lam * a[tq:]
    o = o * lax.rsqrt(jnp.mean(o * o, axis=-1, keepdims=True) + EPS) * subln_ref[...]
    o = o * (1.0 - lam_init)
    g = g_ref[0].astype(jnp.float32)
    out_ref[0] = (o * _silu(g)).astype(out_ref.dtype)


def _diff_attention(proj, lam_vecs, subln, lam_init):
    b, s, _ = proj.shape
    tq = 128
    tk = min(512, s)
    col = lambda c0: (lambda i, h, j: (i, j, c0 // PAIR + h))
    return pl.pallas_call(
        functools.partial(_diff_kernel, tq=tq, tk=tk, lam_init=lam_init),
        grid=(b, D_HEADS, s // tq),
        in_specs=[
            pl.BlockSpec((4, HEAD_DIM), lambda i, h, j: (0, 0)),
            pl.BlockSpec((1, PAIR), lambda i, h, j: (0, 0)),
            pl.BlockSpec((1, tq, PAIR), col(COL_Q2)),
            pl.BlockSpec((1, tq, PAIR), col(COL_G2)),
            pl.BlockSpec((1, s, PAIR), lambda i, h, j: (i, 0, COL_K2 // PAIR + h)),
            pl.BlockSpec((1, s, PAIR), lambda i, h, j: (i, 0, COL_V2 // PAIR + h)),
        ],
        out_specs=pl.BlockSpec((1, tq, PAIR), lambda i, h, j: (i, j, h)),
        out_shape=jax.ShapeDtypeStruct((b, s, 512), jnp.bfloat16),
        compiler_params=pltpu.CompilerParams(
            dimension_semantics=("parallel", "parallel", "parallel"),
            vmem_limit_bytes=VMEM_LIMIT),
        name="diff_attn",
    )(lam_vecs, subln, proj, proj, proj, proj)


def _out_kernel(y1_ref, y2_ref, x_ref, p_ref, w1_ref, w2_ref, gpost_ref, wgate_ref, wple_ref, out_ref):
    mix = (jnp.dot(y1_ref[0], w1_ref[...], preferred_element_type=jnp.float32)
           + jnp.dot(y2_ref[0], w2_ref[...], preferred_element_type=jnp.float32))
    r = lax.rsqrt(jnp.mean(mix * mix, axis=-1, keepdims=True) + EPS)
    x1 = x_ref[0] + mix * r * gpost_ref[...]
    gate = jnp.dot(x1.astype(jnp.bfloat16), wgate_ref[...], preferred_element_type=jnp.float32)
    gate = 1.0 / (1.0 + jnp.exp(-gate))
    emb = jnp.dot(p_ref[0].astype(jnp.bfloat16), wple_ref[...], preferred_element_type=jnp.float32)
    out_ref[0] = x1 + gate * emb


def _out_layer(y1, y2, x, p, layer, w1, w2, gpost, wgate, wple):
    b, s, _ = x.shape
    tm = min(512, s)
    tok = lambda w: pl.BlockSpec((1, tm, w), lambda i, j: (i, j, 0))
    full = lambda a: pl.BlockSpec(a.shape, lambda i, j: (0, 0))
    return pl.pallas_call(
        _out_kernel,
        grid=(b, s // tm),
        in_specs=[tok(512), tok(512), tok(D_MODEL),
                  pl.BlockSpec((None, 1, tm, PLE_DIM), lambda i, j: (layer, i, j, 0)),
                  full(w1), full(w2), full(gpost), full(wgate), full(wple)],
        out_specs=tok(D_MODEL),
        out_shape=jax.ShapeDtypeStruct((b, s, D_MODEL), jnp.float32),
        compiler_params=pltpu.CompilerParams(
            dimension_semantics=("parallel", "parallel"),
            vmem_limit_bytes=VMEM_LIMIT),
        name="out_layer",
    )(y1, y2, x, p, w1, w2, gpost, wgate, wple)


def _head_cols(perm):
    return np.concatenate([np.arange(h * HEAD_DIM, (h + 1) * HEAD_DIM) for h in perm])


def _rope_tables(s):
    n_pair = HEAD_DIM // 4
    inv = (ROPE_THETA ** (-2.0 * np.arange(n_pair) / (HEAD_DIM // 2))).astype(np.float32).astype(np.float64)
    t = np.arange(s)
    ang = np.concatenate([(t // GRID_W)[:, None] * inv, (t % GRID_W)[:, None] * inv], axis=-1)
    cos = np.repeat(np.cos(ang), 2, axis=-1)
    sin = np.repeat(np.sin(ang), 2, axis=-1) * np.tile([-1.0, 1.0], HEAD_DIM // 2)
    cos = np.tile(cos, (1, 2)).astype(np.float32)
    sin = np.tile(sin, (1, 2)).astype(np.float32)
    return jnp.asarray(cos), jnp.asarray(sin)


def _prep_ab(w_in, w_out, sink):
    scale = HEAD_DIM ** -0.5
    qa, ka, va, ga, qb, kb, vb, gb = jnp.split(
        w_in, np.cumsum([512, 128, 128, 512, 512, 512, 512]).tolist(), axis=-1)
    perm = _head_cols(GQ_PERM)
    w = jnp.concatenate([qa[:, perm] * scale, ga[:, perm], qb * scale, kb, vb, gb, ka, va], axis=-1)
    w1 = w_out[:512][perm]
    return w.astype(jnp.bfloat16), w1.astype(jnp.bfloat16), w_out[512:].astype(jnp.bfloat16), sink.astype(jnp.float32)


def _prep_cd(w_in, w_out, q_norm, k_norm):
    scale = HEAD_DIM ** -0.5
    qc, kc, vc, gc, qd, kd, vd, gd = jnp.split(
        w_in, np.cumsum([512, 128, 128, 512, 512, 512, 512]).tolist(), axis=-1)
    perm = _head_cols(GQ_PERM)
    w = jnp.concatenate([qc[:, perm], gc[:, perm], qd * scale, kd, vd, gd, kc, vc], axis=-1)
    w1 = w_out[:512][perm]
    gq = jnp.tile(q_norm.astype(jnp.float32) * scale, C_HEADS)[None]
    gk = jnp.tile(k_norm.astype(jnp.float32), 2)[None]
    return w.astype(jnp.bfloat16), w1.astype(jnp.bfloat16), w_out[512:].astype(jnp.bfloat16), gq, gk


def _trunk(x, p, prm):
    proj = _proj(x, prm["gpre"][0], prm["w_ab"])
    ya = _window_attention(proj, prm["sink"])
    yb = _nbr_attention(proj, prm["nbr_table"])
    x = _out_layer(ya, yb, x, p, 0, prm["w1_ab"], prm["w2_ab"], prm["gpost"][0],
                   prm["wgate"][0], prm["wple"][0])
    cos, sin = prm["rope"]
    proj = _proj(x, prm["gpre"][1], prm["w_cd"], rope=(cos, sin, prm["gq"], prm["gk"]))
    yc = _gqa_attention(proj)
    yd = _diff_attention(proj, prm["lam_vecs"], prm["subln"], prm["lam_init"])
    x = _out_layer(yc, yd, x, p, 1, prm["w1_cd"], prm["w2_cd"], prm["gpost"][1],
                   prm["wgate"][1], prm["wple"][1])
    return x


def kernel(x_prompt, x_sample, p_prompt, p_sample, norm_pre, norm_post, w_ple, w_ple_gate,
           w_in_ab, w_out_ab, a_sink, b_rpb, w_in_cd, w_out_cd, c_q_norm, c_k_norm,
           d_lambda_q1, d_lambda_k1, d_lambda_q2, d_lambda_k2, d_subln):
    assert norm_pre.shape[0] == 2 and w_in_ab.shape[0] == 1 and w_in_cd.shape[0] == 1
    f32 = jnp.float32
    w_ab, w1_ab, w2_ab, sink = _prep_ab(w_in_ab[0], w_out_ab[0], a_sink[0])
    w_cd, w1_cd, w2_cd, gq, gk = _prep_cd(w_in_cd[0], w_out_cd[0], c_q_norm[0], c_k_norm[0])
    prm = dict(
        gpre=norm_pre.astype(f32)[:, None, :], gpost=norm_post.astype(f32)[:, None, :],
        wgate=w_ple_gate.astype(jnp.bfloat16), wple=w_ple.astype(jnp.bfloat16),
        w_ab=w_ab, w1_ab=w1_ab, w2_ab=w2_ab, sink=sink,
        nbr_table=_nbr_bias_table(b_rpb[0]),
        w_cd=w_cd, w1_cd=w1_cd, w2_cd=w2_cd, gq=gq, gk=gk,
        rope=_rope_tables(max(x_prompt.shape[1], x_sample.shape[1])),
        lam_vecs=jnp.stack([d_lambda_q1[0], d_lambda_k1[0], d_lambda_q2[0], d_lambda_k2[0]]).astype(f32),
        subln=d_subln[0].astype(f32)[None, :],
        lam_init=0.8 - 0.6 * math.exp(-0.3 * 1),
    )
    return _trunk(x_prompt, p_prompt, prm), _trunk(x_sample, p_sample, prm)
```
